```python
import jax, jax.numpy as jnp
from jax import lax
import numpy as np


D_MODEL = 1024
BATCH = 16
SEQ = 2048
DEPTH = 4
DEC_BATCH = 2
DEC_SEQ = 16384
PAST_LEN = 128

N_MIXERS = 3
N_MLA_LAYERS = (DEPTH + 2) // 3
N_RET_LAYERS = (DEPTH + 1) // 3
N_POOL_LAYERS = DEPTH // 3
NORM_EPS = 1e-6
ROPE_THETA = 10000.0

MLA_HEADS = 8
MLA_Q_LORA = 384
MLA_KV_LORA = 256
MLA_NOPE = 128
MLA_ROPE = 64
MLA_V = 128
MLA_QK = MLA_NOPE + MLA_ROPE
MLA_SCALE = MLA_QK ** -0.5
Q_BLOCK = 128

RET_HEADS = 4
RET_DK = D_MODEL // RET_HEADS
RET_DV = 2 * D_MODEL // RET_HEADS
RET_CHUNK = 128

POOL_WINDOWS = (2, 4, 8, 16)
N_POOL_GROUPS = len(POOL_WINDOWS)
POOL_GROUP = D_MODEL // N_POOL_GROUPS

D_FF = 4 * D_MODEL

kernel_name = 'hybrid_mla_retention_pool_encoder'


def rmsnorm(x, g):
    x32 = x.astype(jnp.float32)
    r = lax.rsqrt(jnp.mean(x32 * x32, axis=-1, keepdims=True) + NORM_EPS)
    return (x32 * r).astype(x.dtype) * g


def rope(x, theta):
    S, d = x.shape[1], x.shape[-1]
    half = d // 2
    inv = 1.0 / (theta ** (jnp.arange(half, dtype=jnp.float32) * 2.0 / d))
    ang = jnp.arange(S, dtype=jnp.float32)[:, None] * inv[None, :]
    cos = jnp.cos(ang)[:, None, :].astype(x.dtype)
    sin = jnp.sin(ang)[:, None, :].astype(x.dtype)
    x1, x2 = x[..., :half], x[..., half:]
    return jnp.concatenate([x1 * cos - x2 * sin, x1 * sin + x2 * cos], axis=-1)


def mla_mixer(h, w_dq, q_norm, w_uq, w_dkv, kv_norm, w_ukv, w_o):
    B, S, _ = h.shape
    cq = rmsnorm(h @ w_dq, q_norm)
    q = (cq @ w_uq).reshape(B, S, MLA_HEADS, MLA_QK)
    q = jnp.concatenate([q[..., :MLA_NOPE], rope(q[..., MLA_NOPE:], ROPE_THETA)], axis=-1)
    kv_a = h @ w_dkv
    c_kv = rmsnorm(kv_a[..., :MLA_KV_LORA], kv_norm)
    k_pe = rope(kv_a[..., None, MLA_KV_LORA:], ROPE_THETA)
    kv = (c_kv @ w_ukv).reshape(B, S, MLA_HEADS, MLA_NOPE + MLA_V)
    k = jnp.concatenate([kv[..., :MLA_NOPE],
                         jnp.broadcast_to(k_pe, (B, S, MLA_HEADS, MLA_ROPE))], axis=-1)
    v = kv[..., MLA_NOPE:]
    n_blk = S // Q_BLOCK
    qb = q.reshape(B, n_blk, Q_BLOCK, MLA_HEADS, MLA_QK).transpose(1, 0, 2, 3, 4)

    def attend(q_blk):
        s = jnp.einsum('bqhd,bkhd->bhqk', q_blk, k).astype(jnp.float32) * MLA_SCALE
        p = jax.nn.softmax(s, axis=-1).astype(v.dtype)
        return jnp.einsum('bhqk,bkhd->bqhd', p, v)

    o = lax.map(attend, qb)
    o = o.transpose(1, 0, 2, 3, 4).reshape(B, S, MLA_HEADS * MLA_V)
    return o @ w_o


def retention_scan(q, k, v, log_gamma):
    B, S, H, DK = q.shape
    DV = v.shape[-1]
    C = RET_CHUNK
    n = S // C

    def chunks(t):
        return t.reshape(B, n, C, H, t.shape[-1]).transpose(1, 0, 3, 2, 4)

    idx = jnp.arange(C, dtype=jnp.float32)
    diff = idx[:, None] - idx[None, :]
    lg = log_gamma[:, None, None]
    intra = jnp.where(diff >= 0, jnp.exp(jnp.maximum(diff, 0.0) * lg), 0.0).astype(q.dtype)
    q_dec = jnp.exp((idx + 1.0)[None, :] * log_gamma[:, None]).astype(q.dtype)[:, :, None]
    k_dec = jnp.exp((C - 1.0 - idx)[None, :] * log_gamma[:, None]).astype(q.dtype)[:, :, None]
    c_dec = jnp.exp(C * log_gamma).astype(q.dtype)[:, None, None]

    def step(state, blk):
        qc, kc, vc = blk
        att = jnp.einsum('bhid,bhjd->bhij', qc, kc) * intra
        out = (jnp.einsum('bhij,bhjv->bhiv', att, vc)
               + jnp.einsum('bhid,bhdv->bhiv', qc * q_dec, state))
        state = state * c_dec + jnp.einsum('bhjd,bhjv->bhdv', kc * k_dec, vc)
        return state, out

    init = jnp.zeros((B, H, DK, DV), q.dtype)
    _, out = lax.scan(step, init, (chunks(q), chunks(k), chunks(v)))
    return out.transpose(1, 0, 3, 2, 4).reshape(B, S, H, DV)


def retention_mixer(h, w_q, w_k, w_v, w_g, w_o, decay_fwd, decay_bwd):
    B, S, _ = h.shape
    q = rope((h @ w_q).reshape(B, S, RET_HEADS, RET_DK), ROPE_THETA)
    k = rope((h @ w_k).reshape(B, S, RET_HEADS, RET_DK), ROPE_THETA) * (RET_DK ** -0.5)
    v = (h @ w_v).reshape(B, S, RET_HEADS, RET_DV)
    lg_f = -jax.nn.softplus(decay_fwd.astype(jnp.float32))
    lg_b = -jax.nn.softplus(decay_bwd.astype(jnp.float32))
    o_f = retention_scan(q, k, v, lg_f)
    o_b = jnp.flip(retention_scan(jnp.flip(q, 1), jnp.flip(k, 1), jnp.flip(v, 1), lg_b), 1)
    o32 = (o_f + o_b).astype(jnp.float32)
    mu = jnp.mean(o32, axis=-1, keepdims=True)
    var = jnp.mean(jnp.square(o32 - mu), axis=-1, keepdims=True)
    o = ((o32 - mu) * lax.rsqrt(var + NORM_EPS)).astype(h.dtype).reshape(B, S, RET_HEADS * RET_DV)
    gate = jax.nn.silu(h @ w_g)
    return (gate * o) @ w_o


def pool_mixer(h, w_pool, scale):
    B, S, D = h.shape
    hg = h.reshape(B, S, N_POOL_GROUPS, POOL_GROUP)
    cs = jnp.cumsum(hg.astype(jnp.float32), axis=1)
    cs = jnp.concatenate([jnp.zeros((B, 1, N_POOL_GROUPS, POOL_GROUP), jnp.float32), cs], axis=1)
    t = jnp.arange(S)
    outs = []
    for g, w in enumerate(POOL_WINDOWS):
        lo = jnp.clip(t - w // 2, 0, S - 1)
        hi = jnp.clip(t + w // 2 - 1, 0, S - 1)
        csg = cs[:, :, g]
        total = jnp.take(csg, hi + 1, axis=1) - jnp.take(csg, lo, axis=1)
        cnt = (hi - lo + 1).astype(jnp.float32)
        outs.append(total / cnt[None, :, None])
    pooled = jnp.stack(outs, axis=2).astype(h.dtype) - hg
    mixed = jnp.einsum('bsgc,gcd->bsgd', pooled, w_pool).reshape(B, S, D)
    return mixed * scale


def sq_relu_mlp(h, w1, w2):
    return jnp.square(jax.nn.relu(h @ w1)) @ w2


def trunk(x, norm_mix, norm_ffn, mla_w_dq, mla_q_norm, mla_w_uq, mla_w_dkv, mla_kv_norm,
          mla_w_ukv, mla_w_o, ret_w_q, ret_w_k, ret_w_v, ret_w_g, ret_w_o, ret_decay_fwd,
          ret_decay_bwd, pool_w, pool_scale, mlp_w1, mlp_w2, final_norm):
    for i in range(DEPTH):
        kind, j = i % N_MIXERS, i // N_MIXERS
        h = rmsnorm(x, norm_mix[i])
        if kind == 0:
            x = x + mla_mixer(h, mla_w_dq[j], mla_q_norm[j], mla_w_uq[j], mla_w_dkv[j],
                              mla_kv_norm[j], mla_w_ukv[j], mla_w_o[j])
        elif kind == 1:
            x = x + retention_mixer(h, ret_w_q[j], ret_w_k[j], ret_w_v[j], ret_w_g[j],
                                    ret_w_o[j], ret_decay_fwd[j], ret_decay_bwd[j])
        else:
            x = x + pool_mixer(h, pool_w[j], pool_scale[j])
        x = x + sq_relu_mlp(rmsnorm(x, norm_ffn[i]), mlp_w1[i], mlp_w2[i])
    return rmsnorm(x, final_norm)


def setup_inputs(seed: int = 0) -> dict:
    key = jax.random.key(seed)
    ks = jax.random.split(key, 24)
    f32 = jnp.float32

    def nrm(k, shape, fan_in):
        return jax.random.normal(k, shape, f32) * (fan_in ** -0.5)

    def gain(k, shape):
        return 1.0 + 0.02 * jax.random.normal(k, shape, f32)

    target = -np.log(1.0 - 2.0 ** (-5.0 - np.arange(RET_HEADS)))
    raw = jnp.asarray(np.log(np.expm1(target)), dtype=f32)
    return {
        'x_prompt': jax.random.normal(ks[0], (BATCH, SEQ, D_MODEL), f32),
        'x_sample': jax.random.normal(ks[1], (DEC_BATCH, DEC_SEQ, D_MODEL), f32),
        'norm_mix': gain(ks[2], (DEPTH, D_MODEL)),
        'norm_ffn': gain(ks[3], (DEPTH, D_MODEL)),
        'mla_w_dq': nrm(ks[4], (N_MLA_LAYERS, D_MODEL, MLA_Q_LORA), D_MODEL),
        'mla_q_norm': gain(ks[5], (N_MLA_LAYERS, MLA_Q_LORA)),
        'mla_w_uq': nrm(ks[6], (N_MLA_LAYERS, MLA_Q_LORA, MLA_HEADS * MLA_QK), MLA_Q_LORA),
        'mla_w_dkv': nrm(ks[7], (N_MLA_LAYERS, D_MODEL, MLA_KV_LORA + MLA_ROPE), D_MODEL),
        'mla_kv_norm': gain(ks[8], (N_MLA_LAYERS, MLA_KV_LORA)),
        'mla_w_ukv': nrm(ks[9], (N_MLA_LAYERS, MLA_KV_LORA, MLA_HEADS * (MLA_NOPE + MLA_V)), MLA_KV_LORA),
        'mla_w_o': nrm(ks[10], (N_MLA_LAYERS, MLA_HEADS * MLA_V, D_MODEL), MLA_HEADS * MLA_V),
        'ret_w_q': nrm(ks[11], (N_RET_LAYERS, D_MODEL, D_MODEL), D_MODEL),
        'ret_w_k': nrm(ks[12], (N_RET_LAYERS, D_MODEL, D_MODEL), D_MODEL),
        'ret_w_v': nrm(ks[13], (N_RET_LAYERS, D_MODEL, 2 * D_MODEL), D_MODEL),
        'ret_w_g': nrm(ks[14], (N_RET_LAYERS, D_MODEL, 2 * D_MODEL), D_MODEL),
        'ret_w_o': nrm(ks[15], (N_RET_LAYERS, 2 * D_MODEL, D_MODEL), 2 * D_MODEL),
        'ret_decay_fwd': raw[None, :] + 0.05 * jax.random.normal(ks[16], (N_RET_LAYERS, RET_HEADS), f32),
        'ret_decay_bwd': raw[None, :] + 0.05 * jax.random.normal(ks[17], (N_RET_LAYERS, RET_HEADS), f32),
        'pool_w': nrm(ks[18], (N_POOL_LAYERS, N_POOL_GROUPS, POOL_GROUP, POOL_GROUP), POOL_GROUP),
        'pool_scale': gain(ks[19], (N_POOL_LAYERS, D_MODEL)),
        'mlp_w1': nrm(ks[20], (DEPTH, D_MODEL, D_FF), D_MODEL),
        'mlp_w2': nrm(ks[21], (DEPTH, D_FF, D_MODEL), D_FF),
        'final_norm': gain(ks[22], (D_MODEL,)),
    }


def reference(x_prompt, x_sample, norm_mix, norm_ffn, mla_w_dq, mla_q_norm, mla_w_uq,
              mla_w_dkv, mla_kv_norm, mla_w_ukv, mla_w_o, ret_w_q, ret_w_k, ret_w_v, ret_w_g,
              ret_w_o, ret_decay_fwd, ret_decay_bwd, pool_w, pool_scale, mlp_w1, mlp_w2,
              final_norm):
    y_prompt = trunk(x_prompt, norm_mix, norm_ffn, mla_w_dq, mla_q_norm, mla_w_uq, mla_w_dkv,
                     mla_kv_norm, mla_w_ukv, mla_w_o, ret_w_q, ret_w_k, ret_w_v, ret_w_g, ret_w_o,
                     ret_decay_fwd, ret_decay_bwd, pool_w, pool_scale, mlp_w1, mlp_w2, final_norm)
    y_sample = trunk(x_sample, norm_mix, norm_ffn, mla_w_dq, mla_q_norm, mla_w_uq, mla_w_dkv,
                     mla_kv_norm, mla_w_ukv, mla_w_o, ret_w_q, ret_w_k, ret_w_v, ret_w_g, ret_w_o,
                     ret_decay_fwd, ret_decay_bwd, pool_w, pool_scale, mlp_w1, mlp_w2, final_norm)
    return (y_prompt, y_sample)
```

```python
import functools
import math

import jax
import jax.numpy as jnp
from jax import lax
from jax.experimental import pallas as pl
from jax.experimental.pallas import tpu as pltpu

D_MODEL = 1024
DEPTH = 4
N_MIXERS = 3
NORM_EPS = 1e-6
ROPE_THETA = 10000.0

MLA_HEADS = 8
MLA_Q_LORA = 384
MLA_KV_LORA = 256
MLA_NOPE = 128
MLA_ROPE = 64
MLA_V = 128
MLA_QK = MLA_NOPE + MLA_ROPE
MLA_SCALE = MLA_QK ** -0.5

RET_HEADS = 4
RET_DK = D_MODEL // RET_HEADS
RET_DV = 2 * D_MODEL // RET_HEADS
RET_CHUNK = 128

POOL_WINDOWS = (2, 4, 8, 16)
POOL_GROUP = D_MODEL // len(POOL_WINDOWS)
POOL_HALO = 8

D_FF = 4 * D_MODEL

LANES = 128
V7X_VMEM_LIMIT_BYTES = 56 * 1024 * 1024

TOKEN_TILE = 512
FF_CHUNK = 1024
ATTN_BQ = 1024
ATTN_BK = 512
MLA_HEAD_PAD = 2 * LANES
RET_GROUP_CHUNKS = 8
LOG2E = math.log2(math.e)

BF16 = jnp.bfloat16
F32 = jnp.float32


def _params(*semantics):
    return pltpu.CompilerParams(dimension_semantics=semantics,
                                vmem_limit_bytes=V7X_VMEM_LIMIT_BYTES)


def _resident(shape):
    zeros = (0,) * len(shape)
    return pl.BlockSpec(shape, lambda *_: zeros, pipeline_mode=pl.Buffered(1))


def _rms(x, g):
    r = lax.rsqrt(jnp.mean(x * x, axis=-1, keepdims=True) + NORM_EPS)
    return (x * r) * g


def _dot(a, b):
    return jnp.dot(a, b, preferred_element_type=F32)


def _dot_nt(a, b):
    return lax.dot_general(a, b, (((1,), (1,)), ((), ())), preferred_element_type=F32)


def _dot_tn(a, b):
    return lax.dot_general(a, b, (((0,), (0,)), ((), ())), preferred_element_type=F32)


def _rope_block(u, c, sa, sb):
    return u * c + pltpu.roll(u, 3 * MLA_ROPE // 2, 1) * sa + pltpu.roll(u, MLA_ROPE // 2, 1) * sb


def _mla_proj_kernel(x_ref, g_ref, wa_ref, qn_ref, kvn_ref, wuq_ref, wuk_ref, wuv_ref,
                     c_ref, sa_ref, sb_ref, q_ref, k_ref, v_ref):
    h = _rms(x_ref[...], g_ref[...]).astype(BF16)
    a = _dot(h, wa_ref[...])
    cq = _rms(a[:, :MLA_Q_LORA], qn_ref[...]).astype(BF16)
    ckv = _rms(a[:, MLA_Q_LORA:MLA_Q_LORA + MLA_KV_LORA], kvn_ref[...]).astype(BF16)
    c, sa, sb = c_ref[...], sa_ref[...], sb_ref[...]
    kpe = _rope_block(a[:, MLA_Q_LORA + MLA_KV_LORA:], c, sa, sb).astype(BF16)
    q = _dot(cq, wuq_ref[...])
    kn = _dot(ckv, wuk_ref[...])
    v = _dot(ckv, wuv_ref[...])
    for hh in range(MLA_HEADS):
        qo = hh * MLA_HEAD_PAD
        q_ref[hh, :, :LANES] = q[:, qo:qo + LANES].astype(BF16)
        q_ref[hh, :, LANES:] = _rope_block(q[:, qo + LANES:qo + 2 * LANES], c, sa, sb).astype(BF16)
        k_ref[hh, :, :LANES] = kn[:, hh * LANES:(hh + 1) * LANES].astype(BF16)
        k_ref[hh, :, LANES:] = kpe
        v_ref[hh] = v[:, hh * MLA_V:(hh + 1) * MLA_V].astype(BF16)


def _mla_proj(x, seq, g, wa, qn, kvn, wuq, wuk, wuv, tabs):
    t = x.shape[0]
    tm = min(TOKEN_TILE, seq)
    n_pos = seq // tm
    c, sa, sb = tabs
    tab_spec = pl.BlockSpec((tm, LANES), lambda i: (i % n_pos, 0))
    return pl.pallas_call(
        _mla_proj_kernel,
        grid=(t // tm,),
        in_specs=[pl.BlockSpec((tm, D_MODEL), lambda i: (i, 0)),
                  _resident(g.shape), _resident(wa.shape), _resident(qn.shape),
                  _resident(kvn.shape), _resident(wuq.shape), _resident(wuk.shape),
                  _resident(wuv.shape), tab_spec, tab_spec, tab_spec],
        out_specs=[pl.BlockSpec((MLA_HEADS, tm, MLA_HEAD_PAD), lambda i: (0, i, 0)),
                   pl.BlockSpec((MLA_HEADS, tm, MLA_HEAD_PAD), lambda i: (0, i, 0)),
                   pl.BlockSpec((MLA_HEADS, tm, MLA_V), lambda i: (0, i, 0))],
        out_shape=[jax.ShapeDtypeStruct((MLA_HEADS, t, MLA_HEAD_PAD), BF16),
                   jax.ShapeDtypeStruct((MLA_HEADS, t, MLA_HEAD_PAD), BF16),
                   jax.ShapeDtypeStruct((MLA_HEADS, t, MLA_V), BF16)],
        compiler_params=_params("parallel"),
        name="mla_proj",
    )(x, g, wa, qn, kvn, wuq, wuk, wuv, c, sa, sb)


def _attn_kernel(q_ref, k_ref, v_ref, o_ref, m_ref, l_ref, acc_ref):
    j = pl.program_id(2)

    @pl.when(j == 0)
    def _():
        m_ref[...] = jnp.full(m_ref.shape, -jnp.inf, F32)
        l_ref[...] = jnp.zeros(l_ref.shape, F32)
        acc_ref[...] = jnp.zeros(acc_ref.shape, F32)

    c = MLA_SCALE * LOG2E

    def head(hh, carry):
        s = _dot_nt(q_ref[hh], k_ref[hh])
        m_prev = m_ref[hh]
        m_new = jnp.maximum(m_prev, jnp.max(s, axis=1, keepdims=True))
        p = jnp.exp2((s - m_new[:, :1]) * c)
        alpha = jnp.exp2((m_prev - m_new) * c)
        l_ref[hh] = alpha * l_ref[hh] + jnp.sum(p, axis=1, keepdims=True)
        acc_ref[hh] = alpha * acc_ref[hh] + _dot(p.astype(BF16), v_ref[hh])
        m_ref[hh] = m_new
        return carry

    lax.fori_loop(0, MLA_HEADS, head, 0)

    @pl.when(j == pl.num_programs(2) - 1)
    def _():
        for hh in range(MLA_HEADS):
            o_ref[:, hh * MLA_V:(hh + 1) * MLA_V] = (acc_ref[hh] / l_ref[hh]).astype(BF16)


def _attention(q, k, v, batch, seq):
    bq, bk = min(ATTN_BQ, seq), min(ATTN_BK, seq)
    nq, nk = seq // bq, seq // bk
    t = batch * seq
    return pl.pallas_call(
        _attn_kernel,
        grid=(batch, nq, nk),
        in_specs=[pl.BlockSpec((MLA_HEADS, bq, MLA_HEAD_PAD), lambda b, i, j: (0, b * nq + i, 0)),
                  pl.BlockSpec((MLA_HEADS, bk, MLA_HEAD_PAD), lambda b, i, j: (0, b * nk + j, 0)),
                  pl.BlockSpec((MLA_HEADS, bk, MLA_V), lambda b, i, j: (0, b * nk + j, 0))],
        out_specs=pl.BlockSpec((bq, MLA_HEADS * MLA_V), lambda b, i, j: (b * nq + i, 0)),
        out_shape=jax.ShapeDtypeStruct((t, MLA_HEADS * MLA_V), BF16),
        scratch_shapes=[pltpu.VMEM((MLA_HEADS, bq, LANES), F32),
                        pltpu.VMEM((MLA_HEADS, bq, LANES), F32),
                        pltpu.VMEM((MLA_HEADS, bq, MLA_V), F32)],
        compiler_params=_params("parallel", "parallel", "arbitrary"),
        name="mla_attention",
    )(q, k, v)


def _ret_proj_kernel(x_ref, g_ref, w_ref, cos_ref, sin_ref, q_ref, k_ref, v_ref, gate_ref):
    h = _rms(x_ref[...], g_ref[...]).astype(BF16)
    cos, sin = cos_ref[...], sin_ref[...]
    half = RET_DK // 2

    def rope(y):
        x1, x2 = y[:, :half], y[:, half:]
        return x1 * cos - x2 * sin, x1 * sin + x2 * cos

    k_scale = RET_DK ** -0.5
    for hh in range(RET_HEADS):
        yq = _dot(h, w_ref[:, hh * RET_DK:(hh + 1) * RET_DK])
        r1, r2 = rope(yq)
        q_ref[hh, :, :half] = r1.astype(BF16)
        q_ref[hh, :, half:] = r2.astype(BF16)
        yk = _dot(h, w_ref[:, D_MODEL + hh * RET_DK:D_MODEL + (hh + 1) * RET_DK])
        r1, r2 = rope(yk)
        k_ref[hh, :, :half] = (r1 * k_scale).astype(BF16)
        k_ref[hh, :, half:] = (r2 * k_scale).astype(BF16)
        vo = 2 * D_MODEL + hh * RET_DV
        v_ref[hh] = _dot(h, w_ref[:, vo:vo + RET_DV]).astype(BF16)
        go = 4 * D_MODEL + hh * RET_DV
        yg = _dot(h, w_ref[:, go:go + RET_DV])
        gate_ref[:, hh * RET_DV:(hh + 1) * RET_DV] = (yg * (1.0 / (1.0 + jnp.exp(-yg)))).astype(BF16)


def _ret_proj(x, seq, g, w, cos, sin):
    t = x.shape[0]
    tm = min(TOKEN_TILE, seq)
    n_pos = seq // tm
    tab_spec = pl.BlockSpec((tm, RET_DK // 2), lambda i: (i % n_pos, 0))
    return pl.pallas_call(
        _ret_proj_kernel,
        grid=(t // tm,),
        in_specs=[pl.BlockSpec((tm, D_MODEL), lambda i: (i, 0)),
                  _resident(g.shape), _resident(w.shape), tab_spec, tab_spec],
        out_specs=[pl.BlockSpec((RET_HEADS, tm, RET_DK), lambda i: (0, i, 0)),
                   pl.BlockSpec((RET_HEADS, tm, RET_DK), lambda i: (0, i, 0)),
                   pl.BlockSpec((RET_HEADS, tm, RET_DV), lambda i: (0, i, 0)),
                   pl.BlockSpec((tm, RET_HEADS * RET_DV), lambda i: (i, 0))],
        out_shape=[jax.ShapeDtypeStruct((RET_HEADS, t, RET_DK), BF16),
                   jax.ShapeDtypeStruct((RET_HEADS, t, RET_DK), BF16),
                   jax.ShapeDtypeStruct((RET_HEADS, t, RET_DV), BF16),
                   jax.ShapeDtypeStruct((t, RET_HEADS * RET_DV), BF16)],
        compiler_params=_params("parallel"),
        name="ret_proj",
    )(x, g, w, cos, sin)


def _softplus(x):
    return jnp.maximum(x, 0.0) + jnp.log1p(jnp.exp(-jnp.abs(x)))


def _ret_core_kernel(df_ref, db_ref, qf_ref, kf_ref, vf_ref, qb_ref, kb_ref, vb_ref,
                     of_ref, ob_ref, state_ref, intra_ref, qdec_ref, kdec_ref, cdec_ref):
    c = RET_CHUNK

    @pl.when(pl.program_id(2) == 0)
    def _():
        state_ref[...] = jnp.zeros(state_ref.shape, F32)
        row = lax.broadcasted_iota(jnp.int32, (c, c), 0).astype(F32)
        col = lax.broadcasted_iota(jnp.int32, (c, c), 1).astype(F32)
        lg_f = -_softplus(df_ref[0])
        lg_b = -_softplus(db_ref[0])
        d = row - col
        intra_ref[0] = jnp.where(d >= 0, jnp.exp(jnp.maximum(d, 0.0) * lg_f), 0.0)
        intra_ref[1] = jnp.where(d <= 0, jnp.exp(jnp.maximum(-d, 0.0) * lg_b), 0.0)
        qdec_ref[0] = jnp.exp((row + 1.0) * lg_f)
        qdec_ref[1] = jnp.exp((c - row) * lg_b)
        kdec_ref[0] = jnp.exp((c - 1.0 - row) * lg_f)
        kdec_ref[1] = jnp.exp(row * lg_b)
        cdec_ref[0] = jnp.exp(c * lg_f)
        cdec_ref[1] = jnp.exp(c * lg_b)

    def chunk(d, q_ref, k_ref, v_ref, o_ref, ci):
        rows = pl.ds(ci * c, c)
        qc, kc, vc = q_ref[0, rows, :], k_ref[0, rows, :], v_ref[0, rows, :]
        state = state_ref[d]
        att = _dot_nt(qc, kc) * intra_ref[d]
        qdec = jnp.concatenate([qdec_ref[d]] * (RET_DV // LANES), axis=1)
        o_ref[0, rows, :] = _dot(att.astype(BF16), vc) + qdec * _dot(qc, state.astype(BF16))
        kdec = jnp.concatenate([kdec_ref[d]] * (RET_DK // LANES), axis=1)
        kd = (kc.astype(F32) * kdec).astype(BF16)
        cdec = jnp.concatenate([cdec_ref[d]] * (RET_DV // LANES), axis=1)
        state_ref[d] = state * cdec + _dot_tn(kd, vc)

    n = qf_ref.shape[1] // c
    for ci in range(n):
        chunk(0, qf_ref, kf_ref, vf_ref, of_ref, ci)
        chunk(1, qb_ref, kb_ref, vb_ref, ob_ref, n - 1 - ci)


def _ret_core(q, k, v, dec_f, dec_b, batch, seq):
    rows = min(RET_GROUP_CHUNKS * RET_CHUNK, seq)
    ng = seq // rows
    t = batch * seq

    def fwd(width):
        return pl.BlockSpec((1, rows, width), lambda b, h, g: (h, b * ng + g, 0))

    def bwd(width):
        return pl.BlockSpec((1, rows, width), lambda b, h, g: (h, b * ng + ng - 1 - g, 0))

    dec_spec = pl.BlockSpec((1, 1, LANES), lambda b, h, g: (h, 0, 0))
    c = RET_CHUNK
    return pl.pallas_call(
        _ret_core_kernel,
        grid=(batch, RET_HEADS, ng),
        in_specs=[dec_spec, dec_spec, fwd(RET_DK), fwd(RET_DK), fwd(RET_DV),
                  bwd(RET_DK), bwd(RET_DK), bwd(RET_DV)],
        out_specs=[fwd(RET_DV), bwd(RET_DV)],
        out_shape=[jax.ShapeDtypeStruct((RET_HEADS, t, RET_DV), F32),
                   jax.ShapeDtypeStruct((RET_HEADS, t, RET_DV), F32)],
        scratch_shapes=[pltpu.VMEM((2, RET_DK, RET_DV), F32),
                        pltpu.VMEM((2, c, c), F32),
                        pltpu.VMEM((2, c, LANES), F32),
                        pltpu.VMEM((2, c, LANES), F32),
                        pltpu.VMEM((2, 1, LANES), F32)],
        compiler_params=_params("parallel", "parallel", "arbitrary"),
        name="ret_core",
    )(dec_f, dec_b, q, k, v, q, k, v)


def _mlp_tail(xm, gf_ref, w1_ref, w2_ref, fin_ref, o_ref):
    hn = _rms(xm, gf_ref[...]).astype(BF16)
    mlp = None
    for cc in range(D_FF // FF_CHUNK):
        a = jnp.maximum(_dot(hn, w1_ref[:, cc * FF_CHUNK:(cc + 1) * FF_CHUNK]), 0.0)
        part = _dot((a * a).astype(BF16), w2_ref[cc * FF_CHUNK:(cc + 1) * FF_CHUNK, :])
        mlp = part if mlp is None else mlp + part
    acc = xm + mlp
    if fin_ref is not None:
        acc = _rms(acc, fin_ref[...])
    o_ref[...] = acc


def _post_dense_kernel(final, x_ref, pre_ref, wp_ref, gf_ref, w1_ref, w2_ref, *rest):
    fin_ref, o_ref = (rest[0], rest[1]) if final else (None, rest[0])
    xm = x_ref[...] + _dot(pre_ref[...], wp_ref[...])
    _mlp_tail(xm, gf_ref, w1_ref, w2_ref, fin_ref, o_ref)


def _post_ret_kernel(final, x_ref, of_ref, ob_ref, gate_ref, wp_ref, gf_ref, w1_ref, w2_ref, *rest):
    fin_ref, o_ref = (rest[0], rest[1]) if final else (None, rest[0])
    pre = []
    for hh in range(RET_HEADS):
        o = of_ref[hh] + ob_ref[hh]
        mu = jnp.mean(o, axis=-1, keepdims=True)
        d = o - mu
        var = jnp.mean(d * d, axis=-1, keepdims=True)
        on = d * lax.rsqrt(var + NORM_EPS)
        pre.append((gate_ref[:, hh * RET_DV:(hh + 1) * RET_DV].astype(F32) * on).astype(BF16))
    xm = x_ref[...] + _dot(jnp.concatenate(pre, axis=1), wp_ref[...])
    _mlp_tail(xm, gf_ref, w1_ref, w2_ref, fin_ref, o_ref)


def _post_pool_kernel(final, n_pos, seq, x_ref, xp_ref, xn_ref, gm_ref, wp_ref, sc_ref,
                      gf_ref, w1_ref, w2_ref, *rest):
    if final:
        fin_ref, o_ref, h_ref = rest
    else:
        fin_ref, (o_ref, h_ref) = None, rest
    tm = x_ref.shape[0]
    si = pl.program_id(0) % n_pos
    x = x_ref[...]
    gm = gm_ref[...]
    h = _rms(x, gm)
    h_ref[:POOL_HALO, :] = jnp.where(si > 0, _rms(xp_ref[...], gm), 0.0)
    h_ref[POOL_HALO:POOL_HALO + tm, :] = h
    h_ref[POOL_HALO + tm:, :] = jnp.where(si < n_pos - 1, _rms(xn_ref[...], gm), 0.0)
    pos = si * tm + lax.broadcasted_iota(jnp.int32, (tm, POOL_GROUP), 0)
    mixed = []
    for gi, w in enumerate(POOL_WINDOWS):
        cols = slice(gi * POOL_GROUP, (gi + 1) * POOL_GROUP)
        total = h_ref[POOL_HALO - w // 2:POOL_HALO - w // 2 + tm, cols]
        for d in range(-w // 2 + 1, w // 2):
            total = total + h_ref[POOL_HALO + d:POOL_HALO + d + tm, cols]
        cnt = jnp.minimum(pos + (w // 2 - 1), seq - 1) - jnp.maximum(pos - w // 2, 0) + 1
        pooled = total / cnt.astype(F32) - h[:, cols]
        mixed.append(_dot(pooled.astype(BF16), wp_ref[gi]))
    xm = x + jnp.concatenate(mixed, axis=1) * sc_ref[...]
    _mlp_tail(xm, gf_ref, w1_ref, w2_ref, fin_ref, o_ref)


def _post_call(kern, name, x, seq, streamed, weights, final_norm, scratch=()):
    t = x.shape[0]
    tm = min(TOKEN_TILE, seq)
    x_spec = pl.BlockSpec((tm, D_MODEL), lambda i: (i, 0))
    args = [x] + [a for a, _ in streamed] + list(weights)
    specs = [x_spec] + [s for _, s in streamed] + [_resident(w.shape) for w in weights]
    if final_norm is not None:
        args.append(final_norm)
        specs.append(_resident(final_norm.shape))
    return pl.pallas_call(
        kern,
        grid=(t // tm,),
        in_specs=specs,
        out_specs=x_spec,
        out_shape=jax.ShapeDtypeStruct((t, D_MODEL), F32),
        scratch_shapes=list(scratch),
        compiler_params=_params("parallel"),
        name=name,
    )(*args)


def _rope_tables(seq, d):
    half = d // 2
    inv = 1.0 / (ROPE_THETA ** (jnp.arange(half, dtype=F32) * 2.0 / d))
    ang = jnp.arange(seq, dtype=F32)[:, None] * inv[None, :]
    return jnp.cos(ang), jnp.sin(ang)


def _mla_tables(seq):
    cos, sin = _rope_tables(seq, MLA_ROPE)
    z = jnp.zeros_like(cos)
    return (jnp.concatenate([cos, cos, z, z], axis=1),
            jnp.concatenate([-sin, z, z, z], axis=1),
            jnp.concatenate([z, sin, z, z], axis=1))


def _mla_weights(w_dq, w_dkv, w_uq, w_ukv, w_o):
    pad_a = jnp.zeros((D_MODEL, LANES - MLA_ROPE), F32)
    wa = jnp.concatenate([w_dq, w_dkv, pad_a], axis=1).astype(BF16)
    wuq = w_uq.reshape(MLA_Q_LORA, MLA_HEADS, MLA_QK)
    wuq = jnp.pad(wuq, ((0, 0), (0, 0), (0, MLA_HEAD_PAD - MLA_QK)))
    wuq = wuq.reshape(MLA_Q_LORA, MLA_HEADS * MLA_HEAD_PAD).astype(BF16)
    wukv = w_ukv.reshape(MLA_KV_LORA, MLA_HEADS, MLA_NOPE + MLA_V)
    wuk = wukv[:, :, :MLA_NOPE].reshape(MLA_KV_LORA, MLA_HEADS * MLA_NOPE).astype(BF16)
    wuv = wukv[:, :, MLA_NOPE:].reshape(MLA_KV_LORA, MLA_HEADS * MLA_V).astype(BF16)
    return wa, wuq, wuk, wuv, w_o.astype(BF16)


def _row(v):
    return v.reshape(1, -1)


def _trunk(x3, p):
    batch, seq, _ = x3.shape
    t = batch * seq
    x = x3.reshape(t, D_MODEL)
    tm = min(TOKEN_TILE, seq)
    n_pos = seq // tm
    mla_tabs = _mla_tables(seq)
    ret_cos, ret_sin = _rope_tables(seq, RET_DK)
    for i in range(DEPTH):
        kind, j = i % N_MIXERS, i // N_MIXERS
        final = p["final_norm"] if i == DEPTH - 1 else None
        is_final = final is not None
        gm, gf = _row(p["norm_mix"][i]), _row(p["norm_ffn"][i])
        w1, w2 = p["mlp_w1"][i], p["mlp_w2"][i]
        if kind == 0:
            wa, wuq, wuk, wuv, wo = p["mla"][j]
            q, k, v = _mla_proj(x, seq, gm, wa, _row(p["mla_q_norm"][j]), _row(p["mla_kv_norm"][j]),
                                wuq, wuk, wuv, mla_tabs)
            o = _attention(q, k, v, batch, seq)
            x = _post_call(functools.partial(_post_dense_kernel, is_final), "post_mla", x, seq,
                           [(o, pl.BlockSpec((tm, MLA_HEADS * MLA_V), lambda i: (i, 0)))],
                           [wo, gf, w1, w2], final)
        elif kind == 1:
            w_cat, wo, dec_f, dec_b = p["ret"][j]
            q, k, v, gate = _ret_proj(x, seq, gm, w_cat, ret_cos, ret_sin)
            o_f, o_b = _ret_core(q, k, v, dec_f, dec_b, batch, seq)
            o_spec = pl.BlockSpec((RET_HEADS, tm, RET_DV), lambda i: (0, i, 0))
            x = _post_call(functools.partial(_post_ret_kernel, is_final), "post_ret", x, seq,
                           [(o_f, o_spec), (o_b, o_spec),
                            (gate, pl.BlockSpec((tm, RET_HEADS * RET_DV), lambda i: (i, 0)))],
                           [wo, gf, w1, w2], final)
        else:
            wp, sc = p["pool"][j]
            hb = tm // POOL_HALO
            last = t // POOL_HALO - 1
            prev_spec = pl.BlockSpec((POOL_HALO, D_MODEL), lambda i: (jnp.maximum(i * hb - 1, 0), 0))
            next_spec = pl.BlockSpec((POOL_HALO, D_MODEL), lambda i: (jnp.minimum((i + 1) * hb, last), 0))
            x = _post_call(functools.partial(_post_pool_kernel, is_final, n_pos, seq), "post_pool", x, seq,
                           [(x, prev_spec), (x, next_spec)],
                           [gm, wp, sc, gf, w1, w2], final,
                           scratch=[pltpu.VMEM((tm + 2 * POOL_HALO, D_MODEL), F32)])
    return x.reshape(batch, seq, D_MODEL)


def kernel(x_prompt, x_sample, norm_mix, norm_ffn, mla_w_dq, mla_q_norm, mla_w_uq, mla_w_dkv, mla_kv_norm, mla_w_ukv, mla_w_o, ret_w_q, ret_w_k, ret_w_v, ret_w_g, ret_w_o, ret_decay_fwd, ret_decay_bwd, pool_w, pool_scale, mlp_w1, mlp_w2, final_norm):
    def lanes(dec):
        return jnp.broadcast_to(dec.reshape(RET_HEADS, 1, 1), (RET_HEADS, 1, LANES))

    p = {
        "norm_mix": norm_mix, "norm_ffn": norm_ffn,
        "mla_q_norm": mla_q_norm, "mla_kv_norm": mla_kv_norm,
        "mla": [_mla_weights(mla_w_dq[j], mla_w_dkv[j], mla_w_uq[j], mla_w_ukv[j], mla_w_o[j])
                for j in range(mla_w_dq.shape[0])],
        "ret": [(jnp.concatenate([ret_w_q[j], ret_w_k[j], ret_w_v[j], ret_w_g[j]], axis=1).astype(BF16),
                 ret_w_o[j].astype(BF16), lanes(ret_decay_fwd[j]), lanes(ret_decay_bwd[j]))
                for j in range(ret_w_q.shape[0])],
        "pool": [(pool_w[j].astype(BF16), _row(pool_scale[j])) for j in range(pool_w.shape[0])],
        "mlp_w1": mlp_w1.astype(BF16), "mlp_w2": mlp_w2.astype(BF16),
        "final_norm": _row(final_norm),
    }
    return (_trunk(x_prompt, p), _trunk(x_sample, p))
```

```python
import functools
import math

import jax
import jax.numpy as jnp
from jax import lax
from jax.experimental import pallas as pl
from jax.experimental.pallas import tpu as pltpu

D_MODEL = 1024
DEPTH = 4
N_MIXERS = 3
NORM_EPS = 1e-6
ROPE_THETA = 10000.0

MLA_HEADS = 8
MLA_Q_LORA = 384
MLA_KV_LORA = 256
MLA_NOPE = 128
MLA_ROPE = 64
MLA_V = 128
MLA_QK = MLA_NOPE + MLA_ROPE
MLA_SCALE = MLA_QK ** -0.5

RET_HEADS = 4
RET_DK = D_MODEL // RET_HEADS
RET_DV = 2 * D_MODEL // RET_HEADS
RET_CHUNK = 128

POOL_WINDOWS = (2, 4, 8, 16)
POOL_GROUP = D_MODEL // len(POOL_WINDOWS)
POOL_HALO = 8

D_FF = 4 * D_MODEL

LANES = 128
V7X_VMEM_LIMIT_BYTES = 56 * 1024 * 1024

TOKEN_TILE = 512
FF_CHUNK = 1024
ATTN_BQ = 512
ATTN_BK = 1024
MLA_HEAD_PAD = 2 * LANES
RET_GROUP_CHUNKS = 8
LOG2E = math.log2(math.e)

BF16 = jnp.bfloat16
F32 = jnp.float32


def _params(*semantics):
    return pltpu.CompilerParams(dimension_semantics=semantics,
                                vmem_limit_bytes=V7X_VMEM_LIMIT_BYTES)


def _resident(shape):
    zeros = (0,) * len(shape)
    return pl.BlockSpec(shape, lambda *_: zeros, pipeline_mode=pl.Buffered(1))


def _rms(x, g):
    r = lax.rsqrt(jnp.mean(x * x, axis=-1, keepdims=True) + NORM_EPS)
    return (x * r) * g


def _dot(a, b):
    return jnp.dot(a, b, preferred_element_type=F32)


def _dot_nt(a, b):
    return lax.dot_general(a, b, (((1,), (1,)), ((), ())), preferred_element_type=F32)


def _dot_tn(a, b):
    return lax.dot_general(a, b, (((0,), (0,)), ((), ())), preferred_element_type=F32)


def _rope_block(u, c, sa, sb):
    return u * c + pltpu.roll(u, 3 * MLA_ROPE // 2, 1) * sa + pltpu.roll(u, MLA_ROPE // 2, 1) * sb


def _mla_proj_kernel(x_ref, g_ref, wa_ref, qn_ref, kvn_ref, wuq_ref, wuk_ref, wuv_ref,
                     c_ref, sa_ref, sb_ref, q_ref, k_ref, v_ref):
    h = _rms(x_ref[...], g_ref[...]).astype(BF16)
    a = _dot(h, wa_ref[...])
    cq = _rms(a[:, :MLA_Q_LORA], qn_ref[...]).astype(BF16)
    ckv = _rms(a[:, MLA_Q_LORA:MLA_Q_LORA + MLA_KV_LORA], kvn_ref[...]).astype(BF16)
    c, sa, sb = c_ref[...], sa_ref[...], sb_ref[...]
    kpe = _rope_block(a[:, MLA_Q_LORA + MLA_KV_LORA:], c, sa, sb).astype(BF16)
    q = _dot(cq, wuq_ref[...]) * (MLA_SCALE * LOG2E)
    kn = _dot(ckv, wuk_ref[...])
    v = _dot(ckv, wuv_ref[...])
    for hh in range(MLA_HEADS):
        qo = hh * MLA_HEAD_PAD
        q_ref[hh, :, :LANES] = q[:, qo:qo + LANES].astype(BF16)
        q_ref[hh, :, LANES:] = _rope_block(q[:, qo + LANES:qo + 2 * LANES], c, sa, sb).astype(BF16)
        k_ref[hh, :, :LANES] = kn[:, hh * LANES:(hh + 1) * LANES].astype(BF16)
        k_ref[hh, :, LANES:] = kpe
        v_ref[hh] = v[:, hh * MLA_V:(hh + 1) * MLA_V].astype(BF16)


def _mla_proj(x, seq, g, wa, qn, kvn, wuq, wuk, wuv, tabs):
    t = x.shape[0]
    tm = min(TOKEN_TILE, seq)
    n_pos = seq // tm
    c, sa, sb = tabs
    tab_spec = pl.BlockSpec((tm, LANES), lambda i: (i % n_pos, 0))
    return pl.pallas_call(
        _mla_proj_kernel,
        grid=(t // tm,),
        in_specs=[pl.BlockSpec((tm, D_MODEL), lambda i: (i, 0)),
                  _resident(g.shape), _resident(wa.shape), _resident(qn.shape),
                  _resident(kvn.shape), _resident(wuq.shape), _resident(wuk.shape),
                  _resident(wuv.shape), tab_spec, tab_spec, tab_spec],
        out_specs=[pl.BlockSpec((MLA_HEADS, tm, MLA_HEAD_PAD), lambda i: (0, i, 0)),
                   pl.BlockSpec((MLA_HEADS, tm, MLA_HEAD_PAD), lambda i: (0, i, 0)),
                   pl.BlockSpec((MLA_HEADS, tm, MLA_V), lambda i: (0, i, 0))],
        out_shape=[jax.ShapeDtypeStruct((MLA_HEADS, t, MLA_HEAD_PAD), BF16),
                   jax.ShapeDtypeStruct((MLA_HEADS, t, MLA_HEAD_PAD), BF16),
                   jax.ShapeDtypeStruct((MLA_HEADS, t, MLA_V), BF16)],
        compiler_params=_params("parallel"),
        name="mla_proj",
    )(x, g, wa, qn, kvn, wuq, wuk, wuv, c, sa, sb)


def _attn_kernel(q_ref, k_ref, v_ref, o_ref, m_ref, acc_ref, s_ref):
    j = pl.program_id(2)
    bk = k_ref.shape[1]

    @pl.when(j == 0)
    def _():
        m_ref[...] = jnp.full(m_ref.shape, -jnp.inf, F32)
        acc_ref[...] = jnp.zeros(acc_ref.shape, F32)

    lane = lax.broadcasted_iota(jnp.int32, (bk, LANES), 1)
    ones_col = jnp.where(lane == 0, 1.0, 0.0).astype(BF16)

    s_ref[0] = _dot_nt(q_ref[0], k_ref[0])
    for hh in range(MLA_HEADS):
        if hh + 1 < MLA_HEADS:
            s_ref[(hh + 1) % 2] = _dot_nt(q_ref[hh + 1], k_ref[hh + 1])
        s = s_ref[hh % 2]
        m_prev = m_ref[hh]
        m_new = jnp.maximum(m_prev, jnp.max(s, axis=1, keepdims=True))
        p = jnp.exp2(s - m_new[:, :1])
        alpha = jnp.exp2(m_prev - m_new)
        pv = _dot(p.astype(BF16), jnp.concatenate([v_ref[hh], ones_col], axis=1))
        acc_ref[hh, :, :MLA_V] = alpha * acc_ref[hh, :, :MLA_V] + pv[:, :MLA_V]
        acc_ref[hh, :, MLA_V:] = alpha * acc_ref[hh, :, MLA_V:] + pv[:, MLA_V:]
        m_ref[hh] = m_new

    @pl.when(j == pl.num_programs(2) - 1)
    def _():
        for hh in range(MLA_HEADS):
            o_ref[:, hh * MLA_V:(hh + 1) * MLA_V] = (
                acc_ref[hh, :, :MLA_V] / acc_ref[hh, :, MLA_V:MLA_V + 1]).astype(BF16)


def _attention(q, k, v, batch, seq):
    bq, bk = min(ATTN_BQ, seq), min(ATTN_BK, seq)
    nq, nk = seq // bq, seq // bk
    t = batch * seq
    return pl.pallas_call(
        _attn_kernel,
        grid=(batch, nq, nk),
        in_specs=[pl.BlockSpec((MLA_HEADS, bq, MLA_HEAD_PAD), lambda b, i, j: (0, b * nq + i, 0)),
                  pl.BlockSpec((MLA_HEADS, bk, MLA_HEAD_PAD), lambda b, i, j: (0, b * nk + j, 0)),
                  pl.BlockSpec((MLA_HEADS, bk, MLA_V), lambda b, i, j: (0, b * nk + j, 0))],
        out_specs=pl.BlockSpec((bq, MLA_HEADS * MLA_V), lambda b, i, j: (b * nq + i, 0)),
        out_shape=jax.ShapeDtypeStruct((t, MLA_HEADS * MLA_V), BF16),
        scratch_shapes=[pltpu.VMEM((MLA_HEADS, bq, LANES), F32),
                        pltpu.VMEM((MLA_HEADS, bq, MLA_V + LANES), F32),
                        pltpu.VMEM((2, bq, bk), F32)],
        compiler_params=_params("parallel", "parallel", "arbitrary"),
        name="mla_attention",
    )(q, k, v)


def _ret_proj_kernel(x_ref, g_ref, w_ref, cos_ref, sin_ref, q_ref, k_ref, v_ref, gate_ref):
    h = _rms(x_ref[...], g_ref[...]).astype(BF16)
    cos, sin = cos_ref[...], sin_ref[...]
    half = RET_DK // 2

    def rope(y):
        x1, x2 = y[:, :half], y[:, half:]
        return x1 * cos - x2 * sin, x1 * sin + x2 * cos

    k_scale = RET_DK ** -0.5
    for hh in range(RET_HEADS):
        yq = _dot(h, w_ref[:, hh * RET_DK:(hh + 1) * RET_DK])
        r1, r2 = rope(yq)
        q_ref[hh, :, :half] = r1.astype(BF16)
        q_ref[hh, :, half:] = r2.astype(BF16)
        yk = _dot(h, w_ref[:, D_MODEL + hh * RET_DK:D_MODEL + (hh + 1) * RET_DK])
        r1, r2 = rope(yk)
        k_ref[hh, :, :half] = (r1 * k_scale).astype(BF16)
        k_ref[hh, :, half:] = (r2 * k_scale).astype(BF16)
        vo = 2 * D_MODEL + hh * RET_DV
        v_ref[hh] = _dot(h, w_ref[:, vo:vo + RET_DV]).astype(BF16)
        go = 4 * D_MODEL + hh * RET_DV
        yg = _dot(h, w_ref[:, go:go + RET_DV])
        gate_ref[:, hh * RET_DV:(hh + 1) * RET_DV] = (yg * (1.0 / (1.0 + jnp.exp(-yg)))).astype(BF16)


def _ret_proj(x, seq, g, w, cos, sin):
    t = x.shape[0]
    tm = min(TOKEN_TILE, seq)
    n_pos = seq // tm
    tab_spec = pl.BlockSpec((tm, RET_DK // 2), lambda i: (i % n_pos, 0))
    return pl.pallas_call(
        _ret_proj_kernel,
        grid=(t // tm,),
        in_specs=[pl.BlockSpec((tm, D_MODEL), lambda i: (i, 0)),
                  _resident(g.shape), _resident(w.shape), tab_spec, tab_spec],
        out_specs=[pl.BlockSpec((RET_HEADS, tm, RET_DK), lambda i: (0, i, 0)),
                   pl.BlockSpec((RET_HEADS, tm, RET_DK), lambda i: (0, i, 0)),
                   pl.BlockSpec((RET_HEADS, tm, RET_DV), lambda i: (0, i, 0)),
                   pl.BlockSpec((tm, RET_HEADS * RET_DV), lambda i: (i, 0))],
        out_shape=[jax.ShapeDtypeStruct((RET_HEADS, t, RET_DK), BF16),
                   jax.ShapeDtypeStruct((RET_HEADS, t, RET_DK), BF16),
                   jax.ShapeDtypeStruct((RET_HEADS, t, RET_DV), BF16),
                   jax.ShapeDtypeStruct((t, RET_HEADS * RET_DV), BF16)],
        compiler_params=_params("parallel"),
        name="ret_proj",
    )(x, g, w, cos, sin)


def _softplus(x):
    return jnp.maximum(x, 0.0) + jnp.log1p(jnp.exp(-jnp.abs(x)))


def _ret_core_kernel(df_ref, db_ref, qf_ref, kf_ref, vf_ref, qb_ref, kb_ref, vb_ref,
                     of_ref, ob_ref, state_ref, intra_ref, qdec_ref, kdec_ref, cdec_ref):
    c = RET_CHUNK

    @pl.when(pl.program_id(2) == 0)
    def _():
        state_ref[...] = jnp.zeros(state_ref.shape, F32)
        row = lax.broadcasted_iota(jnp.int32, (c, c), 0).astype(F32)
        col = lax.broadcasted_iota(jnp.int32, (c, c), 1).astype(F32)
        lg_f = -_softplus(df_ref[0])
        lg_b = -_softplus(db_ref[0])
        d = row - col
        intra_ref[0] = jnp.where(d >= 0, jnp.exp(jnp.maximum(d, 0.0) * lg_f), 0.0)
        intra_ref[1] = jnp.where(d <= 0, jnp.exp(jnp.maximum(-d, 0.0) * lg_b), 0.0)
        qdec_ref[0] = jnp.exp((row + 1.0) * lg_f)
        qdec_ref[1] = jnp.exp((c - row) * lg_b)
        kdec_ref[0] = jnp.exp((c - 1.0 - row) * lg_f)
        kdec_ref[1] = jnp.exp(row * lg_b)
        cdec_ref[0] = jnp.exp(c * lg_f)
        cdec_ref[1] = jnp.exp(c * lg_b)

    def chunk(d, q_ref, k_ref, v_ref, o_ref, ci):
        rows = pl.ds(ci * c, c)
        qc, kc, vc = q_ref[0, rows, :], k_ref[0, rows, :], v_ref[0, rows, :]
        state = state_ref[d]
        att = _dot_nt(qc, kc) * intra_ref[d]
        qdec = jnp.concatenate([qdec_ref[d]] * (RET_DV // LANES), axis=1)
        o_ref[0, rows, :] = _dot(att.astype(BF16), vc) + qdec * _dot(qc, state.astype(BF16))
        kdec = jnp.concatenate([kdec_ref[d]] * (RET_DK // LANES), axis=1)
        kd = (kc.astype(F32) * kdec).astype(BF16)
        cdec = jnp.concatenate([cdec_ref[d]] * (RET_DV // LANES), axis=1)
        state_ref[d] = state * cdec + _dot_tn(kd, vc)

    n = qf_ref.shape[1] // c
    for ci in range(n):
        chunk(0, qf_ref, kf_ref, vf_ref, of_ref, ci)
        chunk(1, qb_ref, kb_ref, vb_ref, ob_ref, n - 1 - ci)


def _ret_core(q, k, v, dec_f, dec_b, batch, seq):
    rows = min(RET_GROUP_CHUNKS * RET_CHUNK, seq)
    ng = seq // rows
    t = batch * seq

    def fwd(width):
        return pl.BlockSpec((1, rows, width), lambda b, h, g: (h, b * ng + g, 0))

    def bwd(width):
        return pl.BlockSpec((1, rows, width), lambda b, h, g: (h, b * ng + ng - 1 - g, 0))

    dec_spec = pl.BlockSpec((1, 1, LANES), lambda b, h, g: (h, 0, 0))
    c = RET_CHUNK
    return pl.pallas_call(
        _ret_core_kernel,
        grid=(batch, RET_HEADS, ng),
        in_specs=[dec_spec, dec_spec, fwd(RET_DK), fwd(RET_DK), fwd(RET_DV),
                  bwd(RET_DK), bwd(RET_DK), bwd(RET_DV)],
        out_specs=[fwd(RET_DV), bwd(RET_DV)],
        out_shape=[jax.ShapeDtypeStruct((RET_HEADS, t, RET_DV), F32),
                   jax.ShapeDtypeStruct((RET_HEADS, t, RET_DV), F32)],
        scratch_shapes=[pltpu.VMEM((2, RET_DK, RET_DV), F32),
                        pltpu.VMEM((2, c, c), F32),
                        pltpu.VMEM((2, c, LANES), F32),
                        pltpu.VMEM((2, c, LANES), F32),
                        pltpu.VMEM((2, 1, LANES), F32)],
        compiler_params=_params("parallel", "parallel", "arbitrary"),
        name="ret_core",
    )(dec_f, dec_b, q, k, v, q, k, v)


def _mlp_tail(xm, gf_ref, w1_ref, w2_ref, fin_ref, o_ref):
    hn = _rms(xm, gf_ref[...]).astype(BF16)
    mlp = None
    for cc in range(D_FF // FF_CHUNK):
        a = jnp.maximum(_dot(hn, w1_ref[:, cc * FF_CHUNK:(cc + 1) * FF_CHUNK]), 0.0)
        part = _dot((a * a).astype(BF16), w2_ref[cc * FF_CHUNK:(cc + 1) * FF_CHUNK, :])
        mlp = part if mlp is None else mlp + part
    acc = xm + mlp
    if fin_ref is not None:
        acc = _rms(acc, fin_ref[...])
    o_ref[...] = acc


def _post_dense_kernel(final, x_ref, pre_ref, wp_ref, gf_ref, w1_ref, w2_ref, *rest):
    fin_ref, o_ref = (rest[0], rest[1]) if final else (None, rest[0])
    xm = x_ref[...] + _dot(pre_ref[...], wp_ref[...])
    _mlp_tail(xm, gf_ref, w1_ref, w2_ref, fin_ref, o_ref)


def _post_ret_kernel(final, x_ref, of_ref, ob_ref, gate_ref, wp_ref, gf_ref, w1_ref, w2_ref, *rest):
    fin_ref, o_ref = (rest[0], rest[1]) if final else (None, rest[0])
    pre = []
    for hh in range(RET_HEADS):
        o = of_ref[hh] + ob_ref[hh]
        mu = jnp.mean(o, axis=-1, keepdims=True)
        d = o - mu
        var = jnp.mean(d * d, axis=-1, keepdims=True)
        on = d * lax.rsqrt(var + NORM_EPS)
        pre.append((gate_ref[:, hh * RET_DV:(hh + 1) * RET_DV].astype(F32) * on).astype(BF16))
    xm = x_ref[...] + _dot(jnp.concatenate(pre, axis=1), wp_ref[...])
    _mlp_tail(xm, gf_ref, w1_ref, w2_ref, fin_ref, o_ref)


def _post_pool_kernel(final, n_pos, seq, x_ref, xp_ref, xn_ref, gm_ref, wp_ref, sc_ref,
                      gf_ref, w1_ref, w2_ref, *rest):
    if final:
        fin_ref, o_ref, h_ref = rest
    else:
        fin_ref, (o_ref, h_ref) = None, rest
    tm = x_ref.shape[0]
    si = pl.program_id(0) % n_pos
    x = x_ref[...]
    gm = gm_ref[...]
    h = _rms(x, gm)
    h_ref[:POOL_HALO, :] = jnp.where(si > 0, _rms(xp_ref[...], gm), 0.0)
    h_ref[POOL_HALO:POOL_HALO + tm, :] = h
    h_ref[POOL_HALO + tm:, :] = jnp.where(si < n_pos - 1, _rms(xn_ref[...], gm), 0.0)
    pos = si * tm + lax.broadcasted_iota(jnp.int32, (tm, POOL_GROUP), 0)
    mixed = []
    for gi, w in enumerate(POOL_WINDOWS):
        cols = slice(gi * POOL_GROUP, (gi + 1) * POOL_GROUP)
        total = h_ref[POOL_HALO - w // 2:POOL_HALO - w // 2 + tm, cols]
        for d in range(-w // 2 + 1, w // 2):
            total = total + h_ref[POOL_HALO + d:POOL_HALO + d + tm, cols]
        cnt = jnp.minimum(pos + (w // 2 - 1), seq - 1) - jnp.maximum(pos - w // 2, 0) + 1
        pooled = total / cnt.astype(F32) - h[:, cols]
        mixed.append(_dot(pooled.astype(BF16), wp_ref[gi]))
    xm = x + jnp.concatenate(mixed, axis=1) * sc_ref[...]
    _mlp_tail(xm, gf_ref, w1_ref, w2_ref, fin_ref, o_ref)


def _post_call(kern, name, x, seq, streamed, weights, final_norm, scratch=()):
    t = x.shape[0]
    tm = min(TOKEN_TILE, seq)
    x_spec = pl.BlockSpec((tm, D_MODEL), lambda i: (i, 0))
    args = [x] + [a for a, _ in streamed] + list(weights)
    specs = [x_spec] + [s for _, s in streamed] + [_resident(w.shape) for w in weights]
    if final_norm is not None:
        args.append(final_norm)
        specs.append(_resident(final_norm.shape))
    return pl.pallas_call(
        kern,
        grid=(t // tm,),
        in_specs=specs,
        out_specs=x_spec,
        out_shape=jax.ShapeDtypeStruct((t, D_MODEL), F32),
        scratch_shapes=list(scratch),
        compiler_params=_params("parallel"),
        name=name,
    )(*args)


def _rope_tables(seq, d):
    half = d // 2
    inv = 1.0 / (ROPE_THETA ** (jnp.arange(half, dtype=F32) * 2.0 / d))
    ang = jnp.arange(seq, dtype=F32)[:, None] * inv[None, :]
    return jnp.cos(ang), jnp.sin(ang)


def _mla_tables(seq):
    cos, sin = _rope_tables(seq, MLA_ROPE)
    z = jnp.zeros_like(cos)
    return (jnp.concatenate([cos, cos, z, z], axis=1),
            jnp.concatenate([-sin, z, z, z], axis=1),
            jnp.concatenate([z, sin, z, z], axis=1))


def _mla_weights(w_dq, w_dkv, w_uq, w_ukv, w_o):
    pad_a = jnp.zeros((D_MODEL, LANES - MLA_ROPE), F32)
    wa = jnp.concatenate([w_dq, w_dkv, pad_a], axis=1).astype(BF16)
    wuq = w_uq.reshape(MLA_Q_LORA, MLA_HEADS, MLA_QK)
    wuq = jnp.pad(wuq, ((0, 0), (0, 0), (0, MLA_HEAD_PAD - MLA_QK)))
    wuq = wuq.reshape(MLA_Q_LORA, MLA_HEADS * MLA_HEAD_PAD).astype(BF16)
    wukv = w_ukv.reshape(MLA_KV_LORA, MLA_HEADS, MLA_NOPE + MLA_V)
    wuk = wukv[:, :, :MLA_NOPE].reshape(MLA_KV_LORA, MLA_HEADS * MLA_NOPE).astype(BF16)
    wuv = wukv[:, :, MLA_NOPE:].reshape(MLA_KV_LORA, MLA_HEADS * MLA_V).astype(BF16)
    return wa, wuq, wuk, wuv, w_o.astype(BF16)


def _row(v):
    return v.reshape(1, -1)


def _trunk(x3, p):
    batch, seq, _ = x3.shape
    t = batch * seq
    x = x3.reshape(t, D_MODEL)
    tm = min(TOKEN_TILE, seq)
    n_pos = seq // tm
    mla_tabs = _mla_tables(seq)
    ret_cos, ret_sin = _rope_tables(seq, RET_DK)
    for i in range(DEPTH):
        kind, j = i % N_MIXERS, i // N_MIXERS
        final = p["final_norm"] if i == DEPTH - 1 else None
        is_final = final is not None
        gm, gf = _row(p["norm_mix"][i]), _row(p["norm_ffn"][i])
        w1, w2 = p["mlp_w1"][i], p["mlp_w2"][i]
        if kind == 0:
            wa, wuq, wuk, wuv, wo = p["mla"][j]
            q, k, v = _mla_proj(x, seq, gm, wa, _row(p["mla_q_norm"][j]), _row(p["mla_kv_norm"][j]),
                                wuq, wuk, wuv, mla_tabs)
            o = _attention(q, k, v, batch, seq)
            x = _post_call(functools.partial(_post_dense_kernel, is_final), "post_mla", x, seq,
                           [(o, pl.BlockSpec((tm, MLA_HEADS * MLA_V), lambda i: (i, 0)))],
                           [wo, gf, w1, w2], final)
        elif kind == 1:
            w_cat, wo, dec_f, dec_b = p["ret"][j]
            q, k, v, gate = _ret_proj(x, seq, gm, w_cat, ret_cos, ret_sin)
            o_f, o_b = _ret_core(q, k, v, dec_f, dec_b, batch, seq)
            o_spec = pl.BlockSpec((RET_HEADS, tm, RET_DV), lambda i: (0, i, 0))
            x = _post_call(functools.partial(_post_ret_kernel, is_final), "post_ret", x, seq,
                           [(o_f, o_spec), (o_b, o_spec),
                            (gate, pl.BlockSpec((tm, RET_HEADS * RET_DV), lambda i: (i, 0)))],
                           [wo, gf, w1, w2], final)
        else:
            wp, sc = p["pool"][j]
            hb = tm // POOL_HALO
            last = t // POOL_HALO - 1
            prev_spec = pl.BlockSpec((POOL_HALO, D_MODEL), lambda i: (jnp.maximum(i * hb - 1, 0), 0))
            next_spec = pl.BlockSpec((POOL_HALO, D_MODEL), lambda i: (jnp.minimum((i + 1) * hb, last), 0))
            x = _post_call(functools.partial(_post_pool_kernel, is_final, n_pos, seq), "post_pool", x, seq,
                           [(x, prev_spec), (x, next_spec)],
                           [gm, wp, sc, gf, w1, w2], final,
                           scratch=[pltpu.VMEM((tm + 2 * POOL_HALO, D_MODEL), F32)])
    return x.reshape(batch, seq, D_MODEL)


def kernel(x_prompt, x_sample, norm_mix, norm_ffn, mla_w_dq, mla_q_norm, mla_w_uq, mla_w_dkv, mla_kv_norm, mla_w_ukv, mla_w_o, ret_w_q, ret_w_k, ret_w_v, ret_w_g, ret_w_o, ret_decay_fwd, ret_decay_bwd, pool_w, pool_scale, mlp_w1, mlp_w2, final_norm):
    def lanes(dec):
        return jnp.broadcast_to(dec.reshape(RET_HEADS, 1, 1), (RET_HEADS, 1, LANES))

    p = {
        "norm_mix": norm_mix, "norm_ffn": norm_ffn,
        "mla_q_norm": mla_q_norm, "mla_kv_norm": mla_kv_norm,
        "mla": [_mla_weights(mla_w_dq[j], mla_w_dkv[j], mla_w_uq[j], mla_w_ukv[j], mla_w_o[j])
                for j in range(mla_w_dq.shape[0])],
        "ret": [(jnp.concatenate([ret_w_q[j], ret_w_k[j], ret_w_v[j], ret_w_g[j]], axis=1).astype(BF16),
                 ret_w_o[j].astype(BF16), lanes(ret_decay_fwd[j]), lanes(ret_decay_bwd[j]))
                for j in range(ret_w_q.shape[0])],
        "pool": [(pool_w[j].astype(BF16), _row(pool_scale[j])) for j in range(pool_w.shape[0])],
        "mlp_w1": mlp_w1.astype(BF16), "mlp_w2": mlp_w2.astype(BF16),
        "final_norm": _row(final_norm),
    }
    return (_trunk(x_prompt, p), _trunk(x_sample, p))
```

```python
import functools
import math

import jax
import jax.numpy as jnp
from jax import lax
from jax.experimental import pallas as pl
from jax.experimental.pallas import tpu as pltpu

D_MODEL = 1024
DEPTH = 4
N_MIXERS = 3
NORM_EPS = 1e-6
ROPE_THETA = 10000.0

MLA_HEADS = 8
MLA_Q_LORA = 384
MLA_KV_LORA = 256
MLA_NOPE = 128
MLA_ROPE = 64
MLA_V = 128
MLA_QK = MLA_NOPE + MLA_ROPE
MLA_SCALE = MLA_QK ** -0.5

RET_HEADS = 4
RET_DK = D_MODEL // RET_HEADS
RET_DV = 2 * D_MODEL // RET_HEADS

POOL_WINDOWS = (2, 4, 8, 16)
POOL_GROUP = D_MODEL // len(POOL_WINDOWS)
POOL_HALO = 8

D_FF = 4 * D_MODEL

LANES = 128
V7X_VMEM_LIMIT_BYTES = 56 * 1024 * 1024

TOKEN_TILE = 512
FF_CHUNK = 1024
ATTN_BQ = 512
ATTN_BK = 2048
MLA_HEAD_PAD = 2 * LANES
RET_KERNEL_CHUNK = 256
RET_GROUP_ROWS = 1024
LOG2E = math.log2(math.e)

BF16 = jnp.bfloat16
F32 = jnp.float32


def _params(*semantics):
    return pltpu.CompilerParams(dimension_semantics=semantics,
                                vmem_limit_bytes=V7X_VMEM_LIMIT_BYTES)


def _resident(shape):
    zeros = (0,) * len(shape)
    return pl.BlockSpec(shape, lambda *_: zeros, pipeline_mode=pl.Buffered(1))


def _rms(x, g):
    r = lax.rsqrt(jnp.mean(x * x, axis=-1, keepdims=True) + NORM_EPS)
    return (x * r) * g


def _dot(a, b):
    return jnp.dot(a, b, preferred_element_type=F32)


def _dot_nt(a, b):
    return lax.dot_general(a, b, (((1,), (1,)), ((), ())), preferred_element_type=F32)


def _dot_tn(a, b):
    return lax.dot_general(a, b, (((0,), (0,)), ((), ())), preferred_element_type=F32)


def _rope_block(u, c, sa, sb):
    return u * c + pltpu.roll(u, 3 * MLA_ROPE // 2, 1) * sa + pltpu.roll(u, MLA_ROPE // 2, 1) * sb


def _mla_proj_kernel(x_ref, g_ref, wa_ref, qn_ref, kvn_ref, wuq_ref, wuk_ref, wuv_ref,
                     c_ref, sa_ref, sb_ref, q_ref, k_ref, v_ref):
    h = _rms(x_ref[...], g_ref[...]).astype(BF16)
    a = _dot(h, wa_ref[...])
    cq = _rms(a[:, :MLA_Q_LORA], qn_ref[...]).astype(BF16)
    ckv = _rms(a[:, MLA_Q_LORA:MLA_Q_LORA + MLA_KV_LORA], kvn_ref[...]).astype(BF16)
    c, sa, sb = c_ref[...], sa_ref[...], sb_ref[...]
    kpe = _rope_block(a[:, MLA_Q_LORA + MLA_KV_LORA:], c, sa, sb).astype(BF16)
    q = _dot(cq, wuq_ref[...]) * (MLA_SCALE * LOG2E)
    kn = _dot(ckv, wuk_ref[...])
    v = _dot(ckv, wuv_ref[...])
    for hh in range(MLA_HEADS):
        qo = hh * MLA_HEAD_PAD
        q_ref[hh, :, :LANES] = q[:, qo:qo + LANES].astype(BF16)
        q_ref[hh, :, LANES:] = _rope_block(q[:, qo + LANES:qo + 2 * LANES], c, sa, sb).astype(BF16)
        k_ref[hh, :, :LANES] = kn[:, hh * LANES:(hh + 1) * LANES].astype(BF16)
        k_ref[hh, :, LANES:] = kpe
        v_ref[hh] = v[:, hh * MLA_V:(hh + 1) * MLA_V].astype(BF16)


def _mla_proj(x, seq, g, wa, qn, kvn, wuq, wuk, wuv, tabs):
    t = x.shape[0]
    tm = min(TOKEN_TILE, seq)
    n_pos = seq // tm
    c, sa, sb = tabs
    tab_spec = pl.BlockSpec((tm, LANES), lambda i: (i % n_pos, 0))
    return pl.pallas_call(
        _mla_proj_kernel,
        grid=(t // tm,),
        in_specs=[pl.BlockSpec((tm, D_MODEL), lambda i: (i, 0)),
                  _resident(g.shape), _resident(wa.shape), _resident(qn.shape),
                  _resident(kvn.shape), _resident(wuq.shape), _resident(wuk.shape),
                  _resident(wuv.shape), tab_spec, tab_spec, tab_spec],
        out_specs=[pl.BlockSpec((MLA_HEADS, tm, MLA_HEAD_PAD), lambda i: (0, i, 0)),
                   pl.BlockSpec((MLA_HEADS, tm, MLA_HEAD_PAD), lambda i: (0, i, 0)),
                   pl.BlockSpec((MLA_HEADS, tm, MLA_V), lambda i: (0, i, 0))],
        out_shape=[jax.ShapeDtypeStruct((MLA_HEADS, t, MLA_HEAD_PAD), BF16),
                   jax.ShapeDtypeStruct((MLA_HEADS, t, MLA_HEAD_PAD), BF16),
                   jax.ShapeDtypeStruct((MLA_HEADS, t, MLA_V), BF16)],
        compiler_params=_params("parallel"),
        name="mla_proj",
    )(x, g, wa, qn, kvn, wuq, wuk, wuv, c, sa, sb)


def _attn_kernel(q_ref, k_ref, v_ref, o_ref, m_ref, acc_ref, s_ref):
    j = pl.program_id(2)
    bk = k_ref.shape[1]

    @pl.when(j == 0)
    def _():
        m_ref[...] = jnp.full(m_ref.shape, -jnp.inf, F32)
        acc_ref[...] = jnp.zeros(acc_ref.shape, F32)

    lane = lax.broadcasted_iota(jnp.int32, (bk, LANES), 1)
    ones_col = jnp.where(lane == 0, 1.0, 0.0).astype(BF16)

    s_ref[0] = _dot_nt(q_ref[0], k_ref[0])
    for hh in range(MLA_HEADS):
        if hh + 1 < MLA_HEADS:
            s_ref[(hh + 1) % 2] = _dot_nt(q_ref[hh + 1], k_ref[hh + 1])
        s = s_ref[hh % 2]
        m_prev = m_ref[hh]
        m_new = jnp.maximum(m_prev, jnp.max(s, axis=1, keepdims=True))
        p = jnp.exp2(s - m_new[:, :1])
        alpha = jnp.exp2(m_prev - m_new)
        pv = _dot(p.astype(BF16), jnp.concatenate([v_ref[hh], ones_col], axis=1))
        acc_ref[hh, :, :MLA_V] = alpha * acc_ref[hh, :, :MLA_V] + pv[:, :MLA_V]
        acc_ref[hh, :, MLA_V:] = alpha * acc_ref[hh, :, MLA_V:] + pv[:, MLA_V:]
        m_ref[hh] = m_new

    @pl.when(j == pl.num_programs(2) - 1)
    def _():
        for hh in range(MLA_HEADS):
            o_ref[:, hh * MLA_V:(hh + 1) * MLA_V] = (
                acc_ref[hh, :, :MLA_V] / acc_ref[hh, :, MLA_V:MLA_V + 1]).astype(BF16)


def _attention(q, k, v, batch, seq):
    bq, bk = min(ATTN_BQ, seq), min(ATTN_BK, seq)
    nq, nk = seq // bq, seq // bk
    t = batch * seq
    return pl.pallas_call(
        _attn_kernel,
        grid=(batch, nq, nk),
        in_specs=[pl.BlockSpec((MLA_HEADS, bq, MLA_HEAD_PAD), lambda b, i, j: (0, b * nq + i, 0)),
                  pl.BlockSpec((MLA_HEADS, bk, MLA_HEAD_PAD), lambda b, i, j: (0, b * nk + j, 0)),
                  pl.BlockSpec((MLA_HEADS, bk, MLA_V), lambda b, i, j: (0, b * nk + j, 0))],
        out_specs=pl.BlockSpec((bq, MLA_HEADS * MLA_V), lambda b, i, j: (b * nq + i, 0)),
        out_shape=jax.ShapeDtypeStruct((t, MLA_HEADS * MLA_V), BF16),
        scratch_shapes=[pltpu.VMEM((MLA_HEADS, bq, LANES), F32),
                        pltpu.VMEM((MLA_HEADS, bq, MLA_V + LANES), F32),
                        pltpu.VMEM((2, bq, bk), F32)],
        compiler_params=_params("parallel", "parallel", "arbitrary"),
        name="mla_attention",
    )(q, k, v)


def _ret_proj_kernel(x_ref, g_ref, w_ref, cos_ref, sin_ref, q_ref, k_ref, v_ref, gate_ref):
    h = _rms(x_ref[...], g_ref[...]).astype(BF16)
    cos, sin = cos_ref[...], sin_ref[...]
    half = RET_DK // 2

    def rope(y):
        x1, x2 = y[:, :half], y[:, half:]
        return x1 * cos - x2 * sin, x1 * sin + x2 * cos

    k_scale = RET_DK ** -0.5
    for hh in range(RET_HEADS):
        yq = _dot(h, w_ref[:, hh * RET_DK:(hh + 1) * RET_DK])
        r1, r2 = rope(yq)
        q_ref[hh, :, :half] = r1.astype(BF16)
        q_ref[hh, :, half:] = r2.astype(BF16)
        yk = _dot(h, w_ref[:, D_MODEL + hh * RET_DK:D_MODEL + (hh + 1) * RET_DK])
        r1, r2 = rope(yk)
        k_ref[hh, :, :half] = (r1 * k_scale).astype(BF16)
        k_ref[hh, :, half:] = (r2 * k_scale).astype(BF16)
        vo = 2 * D_MODEL + hh * RET_DV
        v_ref[hh] = _dot(h, w_ref[:, vo:vo + RET_DV]).astype(BF16)
        go = 4 * D_MODEL + hh * RET_DV
        yg = _dot(h, w_ref[:, go:go + RET_DV])
        gate_ref[:, hh * RET_DV:(hh + 1) * RET_DV] = (yg * (1.0 / (1.0 + jnp.exp(-yg)))).astype(BF16)


def _ret_proj(x, seq, g, w, cos, sin):
    t = x.shape[0]
    tm = min(TOKEN_TILE, seq)
    n_pos = seq // tm
    tab_spec = pl.BlockSpec((tm, RET_DK // 2), lambda i: (i % n_pos, 0))
    return pl.pallas_call(
        _ret_proj_kernel,
        grid=(t // tm,),
        in_specs=[pl.BlockSpec((tm, D_MODEL), lambda i: (i, 0)),
                  _resident(g.shape), _resident(w.shape), tab_spec, tab_spec],
        out_specs=[pl.BlockSpec((RET_HEADS, tm, RET_DK), lambda i: (0, i, 0)),
                   pl.BlockSpec((RET_HEADS, tm, RET_DK), lambda i: (0, i, 0)),
                   pl.BlockSpec((RET_HEADS, tm, RET_DV), lambda i: (0, i, 0)),
                   pl.BlockSpec((tm, RET_HEADS * RET_DV), lambda i: (i, 0))],
        out_shape=[jax.ShapeDtypeStruct((RET_HEADS, t, RET_DK), BF16),
                   jax.ShapeDtypeStruct((RET_HEADS, t, RET_DK), BF16),
                   jax.ShapeDtypeStruct((RET_HEADS, t, RET_DV), BF16),
                   jax.ShapeDtypeStruct((t, RET_HEADS * RET_DV), BF16)],
        compiler_params=_params("parallel"),
        name="ret_proj",
    )(x, g, w, cos, sin)


def _softplus(x):
    return jnp.maximum(x, 0.0) + jnp.log1p(jnp.exp(-jnp.abs(x)))


def _ret_core_kernel(df_ref, db_ref, qf_ref, kf_ref, vf_ref, qb_ref, kb_ref, vb_ref,
                     of_ref, ob_ref, state_ref, intra_ref, qdec_ref, kdec_ref, cdec_ref):
    c = RET_KERNEL_CHUNK

    @pl.when(pl.program_id(2) == 0)
    def _():
        state_ref[...] = jnp.zeros(state_ref.shape, F32)
        row = lax.broadcasted_iota(jnp.int32, (c, c), 0).astype(F32)
        col = lax.broadcasted_iota(jnp.int32, (c, c), 1).astype(F32)
        lg_f = -_softplus(df_ref[0])
        lg_b = -_softplus(db_ref[0])
        lg_fw = jnp.concatenate([lg_f] * (c // LANES), axis=1)
        lg_bw = jnp.concatenate([lg_b] * (c // LANES), axis=1)
        d = row - col
        intra_ref[0] = jnp.where(d >= 0, jnp.exp(jnp.maximum(d, 0.0) * lg_fw), 0.0)
        intra_ref[1] = jnp.where(d <= 0, jnp.exp(jnp.maximum(-d, 0.0) * lg_bw), 0.0)
        rw = row[:, :LANES]
        qdec_ref[0] = jnp.exp((rw + 1.0) * lg_f)
        qdec_ref[1] = jnp.exp((c - rw) * lg_b)
        kdec_ref[0] = jnp.exp((c - 1.0 - rw) * lg_f)
        kdec_ref[1] = jnp.exp(rw * lg_b)
        cdec_ref[0] = jnp.exp(c * lg_f)
        cdec_ref[1] = jnp.exp(c * lg_b)

    def chunk(d, q_ref, k_ref, v_ref, o_ref, ci):
        rows = pl.ds(ci * c, c)
        qc, kc, vc = q_ref[0, rows, :], k_ref[0, rows, :], v_ref[0, rows, :]
        state = state_ref[d]
        att = _dot_nt(qc, kc) * intra_ref[d]
        qdec = jnp.concatenate([qdec_ref[d]] * (RET_DV // LANES), axis=1)
        o_ref[0, rows, :] = _dot(att.astype(BF16), vc) + qdec * _dot(qc, state.astype(BF16))
        kdec = jnp.concatenate([kdec_ref[d]] * (RET_DK // LANES), axis=1)
        kd = (kc.astype(F32) * kdec).astype(BF16)
        cdec = jnp.concatenate([cdec_ref[d]] * (RET_DV // LANES), axis=1)
        state_ref[d] = state * cdec + _dot_tn(kd, vc)

    n = qf_ref.shape[1] // c
    for ci in range(n):
        chunk(0, qf_ref, kf_ref, vf_ref, of_ref, ci)
        chunk(1, qb_ref, kb_ref, vb_ref, ob_ref, n - 1 - ci)


def _ret_core(q, k, v, dec_f, dec_b, batch, seq):
    rows = min(RET_GROUP_ROWS, seq)
    ng = seq // rows
    t = batch * seq

    def fwd(width):
        return pl.BlockSpec((1, rows, width), lambda b, h, g: (h, b * ng + g, 0))

    def bwd(width):
        return pl.BlockSpec((1, rows, width), lambda b, h, g: (h, b * ng + ng - 1 - g, 0))

    dec_spec = pl.BlockSpec((1, 1, LANES), lambda b, h, g: (h, 0, 0))
    c = RET_KERNEL_CHUNK
    return pl.pallas_call(
        _ret_core_kernel,
        grid=(batch, RET_HEADS, ng),
        in_specs=[dec_spec, dec_spec, fwd(RET_DK), fwd(RET_DK), fwd(RET_DV),
                  bwd(RET_DK), bwd(RET_DK), bwd(RET_DV)],
        out_specs=[fwd(RET_DV), bwd(RET_DV)],
        out_shape=[jax.ShapeDtypeStruct((RET_HEADS, t, RET_DV), F32),
                   jax.ShapeDtypeStruct((RET_HEADS, t, RET_DV), F32)],
        scratch_shapes=[pltpu.VMEM((2, RET_DK, RET_DV), F32),
                        pltpu.VMEM((2, c, c), F32),
                        pltpu.VMEM((2, c, LANES), F32),
                        pltpu.VMEM((2, c, LANES), F32),
                        pltpu.VMEM((2, 1, LANES), F32)],
        compiler_params=_params("parallel", "parallel", "arbitrary"),
        name="ret_core",
    )(dec_f, dec_b, q, k, v, q, k, v)


def _mlp_tail(xm, gf_ref, w1_ref, w2_ref, fin_ref, o_ref):
    hn = _rms(xm, gf_ref[...]).astype(BF16)
    mlp = None
    for cc in range(D_FF // FF_CHUNK):
        a = jnp.maximum(_dot(hn, w1_ref[:, cc * FF_CHUNK:(cc + 1) * FF_CHUNK]), 0.0)
        part = _dot((a * a).astype(BF16), w2_ref[cc * FF_CHUNK:(cc + 1) * FF_CHUNK, :])
        mlp = part if mlp is None else mlp + part
    acc = xm + mlp
    if fin_ref is not None:
        acc = _rms(acc, fin_ref[...])
    o_ref[...] = acc


def _post_dense_kernel(final, x_ref, pre_ref, wp_ref, gf_ref, w1_ref, w2_ref, *rest):
    fin_ref, o_ref = (rest[0], rest[1]) if final else (None, rest[0])
    xm = x_ref[...] + _dot(pre_ref[...], wp_ref[...])
    _mlp_tail(xm, gf_ref, w1_ref, w2_ref, fin_ref, o_ref)


def _post_ret_kernel(final, x_ref, of_ref, ob_ref, gate_ref, wp_ref, gf_ref, w1_ref, w2_ref, *rest):
    fin_ref, o_ref = (rest[0], rest[1]) if final else (None, rest[0])
    pre = []
    for hh in range(RET_HEADS):
        o = of_ref[hh] + ob_ref[hh]
        mu = jnp.mean(o, axis=-1, keepdims=True)
        d = o - mu
        var = jnp.mean(d * d, axis=-1, keepdims=True)
        on = d * lax.rsqrt(var + NORM_EPS)
        pre.append((gate_ref[:, hh * RET_DV:(hh + 1) * RET_DV].astype(F32) * on).astype(BF16))
    xm = x_ref[...] + _dot(jnp.concatenate(pre, axis=1), wp_ref[...])
    _mlp_tail(xm, gf_ref, w1_ref, w2_ref, fin_ref, o_ref)


def _post_pool_kernel(final, n_pos, seq, x_ref, xp_ref, xn_ref, gm_ref, wp_ref, sc_ref,
                      gf_ref, w1_ref, w2_ref, *rest):
    if final:
        fin_ref, o_ref, h_ref = rest
    else:
        fin_ref, (o_ref, h_ref) = None, rest
    tm = x_ref.shape[0]
    si = pl.program_id(0) % n_pos
    x = x_ref[...]
    gm = gm_ref[...]
    h = _rms(x, gm)
    h_ref[:POOL_HALO, :] = jnp.where(si > 0, _rms(xp_ref[...], gm), 0.0)
    h_ref[POOL_HALO:POOL_HALO + tm, :] = h
    h_ref[POOL_HALO + tm:, :] = jnp.where(si < n_pos - 1, _rms(xn_ref[...], gm), 0.0)
    pos = si * tm + lax.broadcasted_iota(jnp.int32, (tm, POOL_GROUP), 0)
    mixed = []
    for gi, w in enumerate(POOL_WINDOWS):
        cols = slice(gi * POOL_GROUP, (gi + 1) * POOL_GROUP)
        total = h_ref[POOL_HALO - w // 2:POOL_HALO - w // 2 + tm, cols]
        for d in range(-w // 2 + 1, w // 2):
            total = total + h_ref[POOL_HALO + d:POOL_HALO + d + tm, cols]
        cnt = jnp.minimum(pos + (w // 2 - 1), seq - 1) - jnp.maximum(pos - w // 2, 0) + 1
        pooled = total / cnt.astype(F32) - h[:, cols]
        mixed.append(_dot(pooled.astype(BF16), wp_ref[gi]))
    xm = x + jnp.concatenate(mixed, axis=1) * sc_ref[...]
    _mlp_tail(xm, gf_ref, w1_ref, w2_ref, fin_ref, o_ref)


def _post_call(kern, name, x, seq, streamed, weights, final_norm, scratch=()):
    t = x.shape[0]
    tm = min(TOKEN_TILE, seq)
    x_spec = pl.BlockSpec((tm, D_MODEL), lambda i: (i, 0))
    args = [x] + [a for a, _ in streamed] + list(weights)
    specs = [x_spec] + [s for _, s in streamed] + [_resident(w.shape) for w in weights]
    if final_norm is not None:
        args.append(final_norm)
        specs.append(_resident(final_norm.shape))
    return pl.pallas_call(
        kern,
        grid=(t // tm,),
        in_specs=specs,
        out_specs=x_spec,
        out_shape=jax.ShapeDtypeStruct((t, D_MODEL), F32),
        scratch_shapes=list(scratch),
        compiler_params=_params("parallel"),
        name=name,
    )(*args)


def _rope_tables(seq, d):
    half = d // 2
    inv = 1.0 / (ROPE_THETA ** (jnp.arange(half, dtype=F32) * 2.0 / d))
    ang = jnp.arange(seq, dtype=F32)[:, None] * inv[None, :]
    return jnp.cos(ang), jnp.sin(ang)


def _mla_tables(seq):
    cos, sin = _rope_tables(seq, MLA_ROPE)
    z = jnp.zeros_like(cos)
    return (jnp.concatenate([cos, cos, z, z], axis=1),
            jnp.concatenate([-sin, z, z, z], axis=1),
            jnp.concatenate([z, sin, z, z], axis=1))


def _mla_weights(w_dq, w_dkv, w_uq, w_ukv, w_o):
    pad_a = jnp.zeros((D_MODEL, LANES - MLA_ROPE), F32)
    wa = jnp.concatenate([w_dq, w_dkv, pad_a], axis=1).astype(BF16)
    wuq = w_uq.reshape(MLA_Q_LORA, MLA_HEADS, MLA_QK)
    wuq = jnp.pad(wuq, ((0, 0), (0, 0), (0, MLA_HEAD_PAD - MLA_QK)))
    wuq = wuq.reshape(MLA_Q_LORA, MLA_HEADS * MLA_HEAD_PAD).astype(BF16)
    wukv = w_ukv.reshape(MLA_KV_LORA, MLA_HEADS, MLA_NOPE + MLA_V)
    wuk = wukv[:, :, :MLA_NOPE].reshape(MLA_KV_LORA, MLA_HEADS * MLA_NOPE).astype(BF16)
    wuv = wukv[:, :, MLA_NOPE:].reshape(MLA_KV_LORA, MLA_HEADS * MLA_V).astype(BF16)
    return wa, wuq, wuk, wuv, w_o.astype(BF16)


def _row(v):
    return v.reshape(1, -1)


def _trunk(x3, p):
    batch, seq, _ = x3.shape
    t = batch * seq
    x = x3.reshape(t, D_MODEL)
    tm = min(TOKEN_TILE, seq)
    n_pos = seq // tm
    mla_tabs = _mla_tables(seq)
    ret_cos, ret_sin = _rope_tables(seq, RET_DK)
    for i in range(DEPTH):
        kind, j = i % N_MIXERS, i // N_MIXERS
        final = p["final_norm"] if i == DEPTH - 1 else None
        is_final = final is not None
        gm, gf = _row(p["norm_mix"][i]), _row(p["norm_ffn"][i])
        w1, w2 = p["mlp_w1"][i], p["mlp_w2"][i]
        if kind == 0:
            wa, wuq, wuk, wuv, wo = p["mla"][j]
            q, k, v = _mla_proj(x, seq, gm, wa, _row(p["mla_q_norm"][j]), _row(p["mla_kv_norm"][j]),
                                wuq, wuk, wuv, mla_tabs)
            o = _attention(q, k, v, batch, seq)
            x = _post_call(functools.partial(_post_dense_kernel, is_final), "post_mla", x, seq,
                           [(o, pl.BlockSpec((tm, MLA_HEADS * MLA_V), lambda i: (i, 0)))],
                           [wo, gf, w1, w2], final)
        elif kind == 1:
            w_cat, wo, dec_f, dec_b = p["ret"][j]
            q, k, v, gate = _ret_proj(x, seq, gm, w_cat, ret_cos, ret_sin)
            o_f, o_b = _ret_core(q, k, v, dec_f, dec_b, batch, seq)
            o_spec = pl.BlockSpec((RET_HEADS, tm, RET_DV), lambda i: (0, i, 0))
            x = _post_call(functools.partial(_post_ret_kernel, is_final), "post_ret", x, seq,
                           [(o_f, o_spec), (o_b, o_spec),
                            (gate, pl.BlockSpec((tm, RET_HEADS * RET_DV), lambda i: (i, 0)))],
                           [wo, gf, w1, w2], final)
        else:
            wp, sc = p["pool"][j]
            hb = tm // POOL_HALO
            last = t // POOL_HALO - 1
            prev_spec = pl.BlockSpec((POOL_HALO, D_MODEL), lambda i: (jnp.maximum(i * hb - 1, 0), 0))
            next_spec = pl.BlockSpec((POOL_HALO, D_MODEL), lambda i: (jnp.minimum((i + 1) * hb, last), 0))
            x = _post_call(functools.partial(_post_pool_kernel, is_final, n_pos, seq), "post_pool", x, seq,
                           [(x, prev_spec), (x, next_spec)],
                           [gm, wp, sc, gf, w1, w2], final,
                           scratch=[pltpu.VMEM((tm + 2 * POOL_HALO, D_MODEL), F32)])
    return x.reshape(batch, seq, D_MODEL)


def kernel(x_prompt, x_sample, norm_mix, norm_ffn, mla_w_dq, mla_q_norm, mla_w_uq, mla_w_dkv, mla_kv_norm, mla_w_ukv, mla_w_o, ret_w_q, ret_w_k, ret_w_v, ret_w_g, ret_w_o, ret_decay_fwd, ret_decay_bwd, pool_w, pool_scale, mlp_w1, mlp_w2, final_norm):
    def lanes(dec):
        return jnp.broadcast_to(dec.reshape(RET_HEADS, 1, 1), (RET_HEADS, 1, LANES))

    p = {
        "norm_mix": norm_mix, "norm_ffn": norm_ffn,
        "mla_q_norm": mla_q_norm, "mla_kv_norm": mla_kv_norm,
        "mla": [_mla_weights(mla_w_dq[j], mla_w_dkv[j], mla_w_uq[j], mla_w_ukv[j], mla_w_o[j])
                for j in range(mla_w_dq.shape[0])],
        "ret": [(jnp.concatenate([ret_w_q[j], ret_w_k[j], ret_w_v[j], ret_w_g[j]], axis=1).astype(BF16),
                 ret_w_o[j].astype(BF16), lanes(ret_decay_fwd[j]), lanes(ret_decay_bwd[j]))
                for j in range(ret_w_q.shape[0])],
        "pool": [(pool_w[j].astype(BF16), _row(pool_scale[j])) for j in range(pool_w.shape[0])],
        "mlp_w1": mlp_w1.astype(BF16), "mlp_w2": mlp_w2.astype(BF16),
        "final_norm": _row(final_norm),
    }
    return (_trunk(x_prompt, p), _trunk(x_sample, p))
```
